```python
import jax, jax.numpy as jnp
from jax import lax
import numpy as np

D_MODEL = 1024
BATCH = 8
SEQ = 8192
DEPTH = 2

D_MIX = D_MODEL
D_RNN = D_MIX // 2
RNN_HEADS = 8
RNN_HEAD_DIM = D_RNN // RNN_HEADS
D_POOL = D_MIX // 4
POOL_WINDOWS = (2, 4, 8, 16)
POOL_GROUPS = len(POOL_WINDOWS)
POOL_GROUP_DIM = D_POOL // POOL_GROUPS
D_SGU = D_MIX // 4
SGU_HEADS = 4
SGU_HEAD_DIM = D_SGU // SGU_HEADS
CHUNK = 128
CONV_WIDTH = 4
LRU_C = 8.0
D_IN = 2 * D_RNN + D_POOL + 2 * D_SGU
D_FF = 64 * ((8 * D_MODEL // 3 + 63) // 64)
EPS = 1e-6

kernel_name = "hybrid_rglru_pool_sgu_macaron"


def rmsnorm(x, g):
    x32 = x.astype(jnp.float32)
    y = x32 * lax.rsqrt(jnp.mean(x32 * x32, axis=-1, keepdims=True) + EPS)
    return (y * g.astype(jnp.float32)).astype(x.dtype)


def swiglu(h, w_in, w_out):
    g, u = jnp.split(h @ w_in, 2, axis=-1)
    return (jax.nn.silu(g) * u) @ w_out


def causal_dwconv(x, w, b):
    S = x.shape[1]
    xp = jnp.pad(x, ((0, 0), (CONV_WIDTH - 1, 0), (0, 0)))
    y = b
    for k in range(CONV_WIDTH):
        y = y + xp[:, k:k + S] * w[k]
    return y


def rglru_branch(gate, xa, conv_w, conv_b, w_a, b_a, w_x, b_x, lam):
    B, S, _ = xa.shape
    xc = causal_dwconv(xa, conv_w, conv_b)
    xh = xc.reshape(B, S, RNN_HEADS, RNN_HEAD_DIM)
    r = jax.nn.sigmoid(jnp.einsum('bshi,hij->bshj', xh, w_a) + b_a)
    i = jax.nn.sigmoid(jnp.einsum('bshi,hij->bshj', xh, w_x) + b_x)
    r32 = r.astype(jnp.float32).reshape(B, S, D_RNN)
    i32 = i.astype(jnp.float32).reshape(B, S, D_RNN)
    x32 = xc.astype(jnp.float32)
    log_a = -LRU_C * r32 * jax.nn.softplus(-lam.astype(jnp.float32))
    a = jnp.exp(log_a)
    mult = jnp.sqrt(-jnp.expm1(2.0 * log_a))
    bvals = mult * (i32 * x32)

    def combine(left, right):
        a_l, b_l = left
        a_r, b_r = right
        return a_l * a_r, a_r * b_l + b_r

    _, h = lax.associative_scan(combine, (a, bvals), axis=1)
    return jax.nn.gelu(gate) * h.astype(gate.dtype)


def causal_window_mean(x32, w):
    S = x32.shape[1]
    cs = jnp.cumsum(x32, axis=1)
    prev = jnp.pad(cs, ((0, 0), (w, 0), (0, 0)))[:, :S]
    count = jnp.minimum(jnp.arange(S) + 1, w).astype(jnp.float32)
    return (cs - prev) / count[None, :, None]


def pool_branch(xp, pool_w, pool_scale):
    x32 = xp.astype(jnp.float32)
    outs = []
    for g, w in enumerate(POOL_WINDOWS):
        xg = x32[..., g * POOL_GROUP_DIM:(g + 1) * POOL_GROUP_DIM]
        d = (causal_window_mean(xg, w) - xg).astype(xp.dtype)
        outs.append(d @ pool_w[g])
    return jnp.concatenate(outs, axis=-1) * pool_scale


def sgu_branch(u, v, sgu_norm, sgu_w, sgu_b):
    B, S, _ = u.shape
    u = jax.nn.gelu(u)
    v = rmsnorm(jax.nn.gelu(v), sgu_norm)
    vh = v.reshape(B, S // CHUNK, CHUNK, SGU_HEADS, SGU_HEAD_DIM)
    mask = jnp.tril(jnp.ones((CHUNK, CHUNK), dtype=bool))
    ws = jnp.where(mask[None], sgu_w, jnp.zeros_like(sgu_w))
    z = jnp.einsum('hts,bnshd->bnthd', ws, vh) + jnp.transpose(sgu_b)[None, None, :, :, None]
    return u * z.reshape(B, S, D_SGU)


def setup_inputs(seed: int = 0) -> dict:
    key = jax.random.key(seed)
    ks = jax.random.split(key, 24)
    f32 = jnp.float32

    def nrm(k, shape, scale):
        return jax.random.normal(k, shape, f32) * scale

    def gain(k, shape):
        return 1.0 + 0.05 * jax.random.normal(k, shape, f32)

    u_a = jax.random.uniform(ks[9], (DEPTH, D_RNN), f32, 0.9, 0.999)
    s = u_a ** (1.0 / LRU_C)
    lru_lambda = jnp.log(s) - jnp.log1p(-s)

    return {
        "x": jax.random.normal(ks[0], (BATCH, SEQ, D_MODEL), f32),
        "ffn1_norm": gain(ks[1], (DEPTH, D_MODEL)),
        "ffn1_w_in": nrm(ks[2], (DEPTH, D_MODEL, 2 * D_FF), D_MODEL ** -0.5),
        "ffn1_w_out": nrm(ks[3], (DEPTH, D_FF, D_MODEL), D_FF ** -0.5),
        "mix_norm": gain(ks[4], (DEPTH, D_MODEL)),
        "w_in": nrm(ks[5], (DEPTH, D_MODEL, D_IN), D_MODEL ** -0.5),
        "conv_w": nrm(ks[6], (DEPTH, CONV_WIDTH, D_RNN), CONV_WIDTH ** -0.5),
        "conv_b": nrm(ks[7], (DEPTH, D_RNN), 0.02),
        "rg_w_a": nrm(ks[8], (DEPTH, RNN_HEADS, RNN_HEAD_DIM, RNN_HEAD_DIM), RNN_HEAD_DIM ** -0.5),
        "rg_b_a": nrm(ks[10], (DEPTH, RNN_HEADS, RNN_HEAD_DIM), 0.02),
        "rg_w_x": nrm(ks[11], (DEPTH, RNN_HEADS, RNN_HEAD_DIM, RNN_HEAD_DIM), RNN_HEAD_DIM ** -0.5),
        "rg_b_x": nrm(ks[12], (DEPTH, RNN_HEADS, RNN_HEAD_DIM), 0.02),
        "lru_lambda": lru_lambda,
        "pool_w": nrm(ks[13], (DEPTH, POOL_GROUPS, POOL_GROUP_DIM, POOL_GROUP_DIM), POOL_GROUP_DIM ** -0.5),
        "pool_scale": gain(ks[14], (DEPTH, D_POOL)),
        "sgu_norm": gain(ks[15], (DEPTH, D_SGU)),
        "sgu_w": nrm(ks[16], (DEPTH, SGU_HEADS, CHUNK, CHUNK), CHUNK ** -0.5),
        "sgu_b": gain(ks[17], (DEPTH, SGU_HEADS, CHUNK)),
        "w_out": nrm(ks[18], (DEPTH, D_MIX, D_MODEL), D_MIX ** -0.5),
        "ffn2_norm": gain(ks[19], (DEPTH, D_MODEL)),
        "ffn2_w_in": nrm(ks[20], (DEPTH, D_MODEL, 2 * D_FF), D_MODEL ** -0.5),
        "ffn2_w_out": nrm(ks[21], (DEPTH, D_FF, D_MODEL), D_FF ** -0.5),
        "final_norm": gain(ks[22], (D_MODEL,)),
    }


def reference(x, ffn1_norm, ffn1_w_in, ffn1_w_out, mix_norm, w_in, conv_w, conv_b,
              rg_w_a, rg_b_a, rg_w_x, rg_b_x, lru_lambda, pool_w, pool_scale,
              sgu_norm, sgu_w, sgu_b, w_out, ffn2_norm, ffn2_w_in, ffn2_w_out, final_norm):
    s1 = D_RNN
    s2 = 2 * D_RNN
    s3 = s2 + D_POOL
    s4 = s3 + D_SGU
    for l in range(DEPTH):
        x = x + 0.5 * swiglu(rmsnorm(x, ffn1_norm[l]), ffn1_w_in[l], ffn1_w_out[l])
        h = rmsnorm(x, mix_norm[l])
        p = h @ w_in[l]
        gate_a, xa, xp, u, v = jnp.split(p, [s1, s2, s3, s4], axis=-1)
        ya = rglru_branch(gate_a, xa, conv_w[l], conv_b[l], rg_w_a[l], rg_b_a[l],
                          rg_w_x[l], rg_b_x[l], lru_lambda[l])
        yb = pool_branch(xp, pool_w[l], pool_scale[l])
        yc = sgu_branch(u, v, sgu_norm[l], sgu_w[l], sgu_b[l])
        x = x + jnp.concatenate([ya, yb, yc], axis=-1) @ w_out[l]
        x = x + 0.5 * swiglu(rmsnorm(x, ffn2_norm[l]), ffn2_w_in[l], ffn2_w_out[l])
    return rmsnorm(x, final_norm)
```

```python
import functools
import math

import jax
import jax.numpy as jnp
from jax import lax
from jax.experimental import pallas as pl
from jax.experimental.pallas import tpu as pltpu

EPS = 1e-6
LRU_C = 8.0
CONV_WIDTH = 4
POOL_WINDOWS = (2, 4, 8, 16)
LANES = 128
SUBLANES = 8
MXU_DIM = 256
VMEM_LIMIT_BYTES = 60 * 1024 * 1024

F32 = jnp.float32
BF16 = jnp.bfloat16


def _round_up(n, m):
    return (n + m - 1) // m * m


def _rmsnorm(x, g):
    ms = jnp.mean(x * x, axis=-1, keepdims=True)
    return x * lax.rsqrt(ms + EPS) * g


def _sigmoid(z):
    return 0.5 * jnp.tanh(0.5 * z) + 0.5


def _gelu(x):
    inner = x * (math.sqrt(2.0 / math.pi) * (1.0 + 0.044715 * (x * x)))
    hx = 0.5 * x
    return hx + hx * jnp.tanh(inner)


def _ffn_kernel(x_ref, g_ref, wg_ref, wu_ref, wo_ref, fg_ref, o_ref, a_ref, *, chunk, final_norm):
    x = x_ref[...]
    h = _rmsnorm(x, g_ref[...]).astype(BF16)
    d_ff = wg_ref.shape[1]
    for c in range(d_ff // chunk):
        sl = slice(c * chunk, (c + 1) * chunk)
        g = jnp.dot(h, wg_ref[:, sl], preferred_element_type=F32)
        u = jnp.dot(h, wu_ref[:, sl], preferred_element_type=F32)
        a_ref[:, sl] = (g * _sigmoid(g) * u).astype(BF16)
    y = jnp.dot(a_ref[...], wo_ref[...], preferred_element_type=F32)
    o = x + 0.5 * y
    if final_norm:
        o = _rmsnorm(o, fg_ref[...])
    o_ref[...] = o


def _resident(shape):
    return pl.BlockSpec(shape, lambda i: (0,) * len(shape), pipeline_mode=pl.Buffered(1))


def _ffn(x2d, norm_g, wg, wu, wo, final_g, *, final_norm, block_rows=512, chunk=MXU_DIM):
    n, d = x2d.shape
    d_ff = wg.shape[1]
    return pl.pallas_call(
        functools.partial(_ffn_kernel, chunk=chunk, final_norm=final_norm),
        grid=(n // block_rows,),
        in_specs=[
            pl.BlockSpec((block_rows, d), lambda i: (i, 0)),
            _resident((1, d)),
            _resident((d, d_ff)),
            _resident((d, d_ff)),
            _resident((d_ff, d)),
            _resident((1, d)),
        ],
        out_specs=pl.BlockSpec((block_rows, d), lambda i: (i, 0)),
        out_shape=jax.ShapeDtypeStruct((n, d), F32),
        scratch_shapes=[pltpu.VMEM((block_rows, d_ff), BF16)],
        compiler_params=pltpu.CompilerParams(
            dimension_semantics=("arbitrary",), vmem_limit_bytes=VMEM_LIMIT_BYTES),
        name="ffn_final" if final_norm else "ffn",
    )(x2d, norm_g, wg, wu, wo, final_g)


def _prep_ffn_weights(w_in, w_out):
    d_ff = w_out.shape[0]
    pad = _round_up(d_ff, MXU_DIM) - d_ff
    wg = jnp.pad(w_in[:, :d_ff], ((0, 0), (0, pad))).astype(BF16)
    wu = jnp.pad(w_in[:, d_ff:], ((0, 0), (0, pad))).astype(BF16)
    wo = jnp.pad(w_out, ((0, pad), (0, 0))).astype(BF16)
    return wg, wu, wo


NB = 8
TT = 128
CONV_HALO = SUBLANES
POOL_HALO = 16
XA_PITCH = CONV_HALO + TT
XP_PITCH = POOL_HALO + TT
SCAN_PITCH = TT + 4


def _mixer_kernel(x_ref, g_ref, win_ref, cw_ref, cb_ref, wa_ref, ba_ref, wx_ref, bx_ref, lam_ref,
                  pw_ref, ps_ref, sn_ref, sw_ref, sb_ref, wout_ref, o_ref,
                  h_s, xa_s, xp_s, xc_s, a_s, b_s, hc_s, y_s, wcat_s, vbd_s,
                  *, d_rnn, d_pool, d_sgu):
    i = pl.program_id(0)
    n_rnn = d_rnn // LANES
    n_pool = d_pool // LANES
    rows = NB * TT
    c_gate, c_xa, c_xp, c_u, c_v = 0, d_rnn, 2 * d_rnn, 2 * d_rnn + d_pool, 2 * d_rnn + d_pool + d_sgu

    @pl.when(i == 0)
    def _init():
        for b in range(NB):
            for c in range(n_rnn):
                xa_s[c, b * XA_PITCH:b * XA_PITCH + CONV_HALO, :] = jnp.zeros((CONV_HALO, LANES), F32)
            for c in range(n_pool):
                xp_s[c, b * XP_PITCH:b * XP_PITCH + POOL_HALO, :] = jnp.zeros((POOL_HALO, LANES), F32)
        hc_s[...] = jnp.zeros(hc_s.shape, F32)
        w = sw_ref[...]
        t_idx = lax.broadcasted_iota(jnp.int32, w.shape, 0)
        s_idx = lax.broadcasted_iota(jnp.int32, w.shape, 1) % TT
        wcat_s[...] = jnp.where(t_idx >= s_idx, w, 0.0).astype(BF16)

    g = g_ref[...]
    for b in range(NB):
        h_s[b * TT:(b + 1) * TT, :] = _rmsnorm(x_ref[b], g).astype(BF16)
    h = h_s[...]

    def proj(lo, width):
        return jnp.dot(h, win_ref[:, lo:lo + width], preferred_element_type=F32)

    xa = proj(c_xa, d_rnn)
    for b in range(NB):
        for c in range(n_rnn):
            xa_s[c, b * XA_PITCH + CONV_HALO:(b + 1) * XA_PITCH, :] = xa[b * TT:(b + 1) * TT, c * LANES:(c + 1) * LANES]
    for c in range(n_rnn):
        lanes = slice(c * LANES, (c + 1) * LANES)
        taps = [jnp.broadcast_to(cw_ref[k:k + 1, lanes], (TT, LANES)) for k in range(CONV_WIDTH)]
        bias = jnp.broadcast_to(cb_ref[:, lanes], (TT, LANES))
        for b in range(NB):
            base = b * XA_PITCH + CONV_HALO
            acc = bias
            for k in range(CONV_WIDTH):
                shift = CONV_WIDTH - 1 - k
                acc = acc + taps[k] * xa_s[c, base - shift:base - shift + TT, :]
            xc_s[b * TT:(b + 1) * TT, lanes] = acc
            xa_s[c, b * XA_PITCH:b * XA_PITCH + CONV_HALO, :] = xa_s[c, (b + 1) * XA_PITCH - CONV_HALO:(b + 1) * XA_PITCH, :]
    xc_bf = xc_s[...].astype(BF16)
    za = jnp.dot(xc_bf, wa_ref[...], preferred_element_type=F32)
    zx = jnp.dot(xc_bf, wx_ref[...], preferred_element_type=F32)
    neg_lam = -lam_ref[...]
    softplus = jnp.maximum(neg_lam, 0.0) + jnp.log1p(jnp.exp(-jnp.abs(neg_lam)))
    decay = -LRU_C * softplus
    for c in range(n_rnn):
        lanes = slice(c * LANES, (c + 1) * LANES)
        dec = jnp.broadcast_to(decay[:, lanes], (TT, LANES))
        b_a = jnp.broadcast_to(ba_ref[:, lanes], (TT, LANES))
        b_x = jnp.broadcast_to(bx_ref[:, lanes], (TT, LANES))
        for b in range(NB):
            rs = slice(b * TT, (b + 1) * TT)
            r = _sigmoid(za[rs, lanes] + b_a)
            gate_i = _sigmoid(zx[rs, lanes] + b_x)
            log_a = r * dec
            th = jnp.tanh(log_a)
            mult = jnp.sqrt(-2.0 * th / (1.0 - th))
            a_s[c, b * SCAN_PITCH:b * SCAN_PITCH + TT, :] = jnp.exp(log_a)
            b_s[c, b * SCAN_PITCH:b * SCAN_PITCH + TT, :] = mult * (gate_i * xc_s[rs, lanes])
    state = [hc_s[c] for c in range(n_rnn)]
    for t in range(TT):
        for c in range(n_rnn):
            a_t = a_s[c, pl.ds(t, NB, stride=SCAN_PITCH), :]
            b_t = b_s[c, pl.ds(t, NB, stride=SCAN_PITCH), :]
            state[c] = a_t * state[c] + b_t
            b_s[c, pl.ds(t, NB, stride=SCAN_PITCH), :] = state[c]
    for c in range(n_rnn):
        hc_s[c] = state[c]
    gate = proj(c_gate, d_rnn)
    for b in range(NB):
        for c in range(n_rnn):
            lanes = slice(c * LANES, (c + 1) * LANES)
            hb = b_s[c, b * SCAN_PITCH:b * SCAN_PITCH + TT, :]
            y_s[b * TT:(b + 1) * TT, lanes] = (_gelu(gate[b * TT:(b + 1) * TT, lanes]) * hb).astype(BF16)

    xp = proj(c_xp, d_pool)
    for b in range(NB):
        for c in range(n_pool):
            xp_s[c, b * XP_PITCH + POOL_HALO:(b + 1) * XP_PITCH, :] = xp[b * TT:(b + 1) * TT, c * LANES:(c + 1) * LANES]
    pos = i * TT + lax.broadcasted_iota(jnp.int32, (TT, LANES), 0) + 1
    lane = lax.broadcasted_iota(jnp.int32, (TT, LANES), 1)
    group_dim = LANES // 2
    d_blocks = []
    for c in range(n_pool):
        w_lo, w_hi = POOL_WINDOWS[2 * c], POOL_WINDOWS[2 * c + 1]
        in_lo = lane < group_dim
        window = jnp.where(in_lo, w_lo, w_hi)
        inv_count = 1.0 / jnp.minimum(pos, window).astype(F32)
        rows_c = []
        for b in range(NB):
            base = b * XP_PITCH + POOL_HALO
            x0 = xp_s[c, base:base + TT, :]
            s_lo = x0
            for k in range(1, w_lo):
                s_lo = s_lo + xp_s[c, base - k:base - k + TT, :]
            s_hi = s_lo
            for k in range(w_lo, w_hi):
                s_hi = s_hi + xp_s[c, base - k:base - k + TT, :]
            rows_c.append((jnp.where(in_lo, s_lo, s_hi) * inv_count - x0).astype(BF16))
            xp_s[c, b * XP_PITCH:b * XP_PITCH + POOL_HALO, :] = xp_s[c, (b + 1) * XP_PITCH - POOL_HALO:(b + 1) * XP_PITCH, :]
        d_blocks.append(jnp.concatenate(rows_c, axis=0))
    d_bf = jnp.concatenate(d_blocks, axis=1)
    yb = jnp.dot(d_bf, pw_ref[...], preferred_element_type=F32) * ps_ref[...]
    y_s[:, d_rnn:d_rnn + d_pool] = yb.astype(BF16)

    u = _gelu(proj(c_u, d_sgu))
    v = _rmsnorm(_gelu(proj(c_v, d_sgu)), sn_ref[...])
    n_heads = wcat_s.shape[1] // TT
    head_dim = d_sgu // n_heads
    col_head = lax.broadcasted_iota(jnp.int32, (TT, d_sgu), 1) // head_dim
    wcat = wcat_s[...]
    bias = sb_ref[...]
    for b in range(NB):
        vb = v[b * TT:(b + 1) * TT, :]
        for hd in range(n_heads):
            vbd_s[b, hd * TT:(hd + 1) * TT, :] = jnp.where(col_head == hd, vb, 0.0).astype(BF16)
        z = jnp.dot(wcat, vbd_s[b], preferred_element_type=F32) + bias
        y_s[b * TT:(b + 1) * TT, d_rnn + d_pool:] = (u[b * TT:(b + 1) * TT, :] * z).astype(BF16)

    out = jnp.dot(y_s[...], wout_ref[...], preferred_element_type=F32)
    for b in range(NB):
        o_ref[b] = x_ref[b] + out[b * TT:(b + 1) * TT, :]


def _block_diag(w):
    heads, k, n = w.shape
    eye = jnp.eye(heads, dtype=w.dtype)
    return jnp.einsum('hij,hg->higj', w, eye).reshape(heads * k, heads * n)


def _mixer(x, norm_g, w_in, conv_w, conv_b, rg_w_a, rg_b_a, rg_w_x, rg_b_x, lam, pool_w, pool_scale,
           sgu_norm, sgu_w, sgu_b, w_out):
    b, s, d = x.shape
    assert b == NB and s % TT == 0
    d_rnn = conv_w.shape[1]
    d_pool = pool_scale.shape[0]
    d_sgu = sgu_norm.shape[0]
    n_heads, chunk, _ = sgu_w.shape
    assert chunk == TT
    head_dim = d_sgu // n_heads
    operands = [
        x,
        norm_g[None],
        w_in.astype(BF16),
        conv_w,
        conv_b[None],
        _block_diag(rg_w_a).astype(BF16),
        rg_b_a.reshape(1, d_rnn),
        _block_diag(rg_w_x).astype(BF16),
        rg_b_x.reshape(1, d_rnn),
        lam[None],
        _block_diag(pool_w).astype(BF16),
        pool_scale[None],
        sgu_norm[None],
        jnp.transpose(sgu_w, (1, 0, 2)).reshape(TT, n_heads * TT),
        jnp.repeat(jnp.transpose(sgu_b), head_dim, axis=1),
        w_out.astype(BF16),
    ]
    tile = pl.BlockSpec((NB, TT, d), lambda i: (0, i, 0))
    in_specs = [tile] + [_resident(op.shape) for op in operands[1:]]
    rows = NB * TT
    scratch = [
        pltpu.VMEM((rows, d), BF16),
        pltpu.VMEM((d_rnn // LANES, NB * XA_PITCH, LANES), F32),
        pltpu.VMEM((d_pool // LANES, NB * XP_PITCH, LANES), F32),
        pltpu.VMEM((rows, d_rnn), F32),
        pltpu.VMEM((d_rnn // LANES, NB * SCAN_PITCH, LANES), F32),
        pltpu.VMEM((d_rnn // LANES, NB * SCAN_PITCH, LANES), F32),
        pltpu.VMEM((d_rnn // LANES, NB, LANES), F32),
        pltpu.VMEM((rows, d_rnn + d_pool + d_sgu), BF16),
        pltpu.VMEM((TT, n_heads * TT), BF16),
        pltpu.VMEM((NB, n_heads * TT, d_sgu), BF16),
    ]
    return pl.pallas_call(
        functools.partial(_mixer_kernel, d_rnn=d_rnn, d_pool=d_pool, d_sgu=d_sgu),
        grid=(s // TT,),
        in_specs=in_specs,
        out_specs=tile,
        out_shape=jax.ShapeDtypeStruct(x.shape, F32),
        scratch_shapes=scratch,
        compiler_params=pltpu.CompilerParams(
            dimension_semantics=("arbitrary",), vmem_limit_bytes=VMEM_LIMIT_BYTES),
        name="mixer",
    )(*operands)


def kernel(x, ffn1_norm, ffn1_w_in, ffn1_w_out, mix_norm, w_in, conv_w, conv_b, rg_w_a, rg_b_a, rg_w_x, rg_b_x, lru_lambda, pool_w, pool_scale, sgu_norm, sgu_w, sgu_b, w_out, ffn2_norm, ffn2_w_in, ffn2_w_out, final_norm):
    b, s, d = x.shape
    depth = ffn1_norm.shape[0]
    fg = final_norm[None]
    for l in range(depth):
        wg, wu, wo = _prep_ffn_weights(ffn1_w_in[l], ffn1_w_out[l])
        x = _ffn(x.reshape(b * s, d), ffn1_norm[l][None], wg, wu, wo, fg, final_norm=False).reshape(b, s, d)
        x = _mixer(x, mix_norm[l], w_in[l], conv_w[l], conv_b[l], rg_w_a[l], rg_b_a[l], rg_w_x[l], rg_b_x[l],
                   lru_lambda[l], pool_w[l], pool_scale[l], sgu_norm[l], sgu_w[l], sgu_b[l], w_out[l])
        wg, wu, wo = _prep_ffn_weights(ffn2_w_in[l], ffn2_w_out[l])
        x = _ffn(x.reshape(b * s, d), ffn2_norm[l][None], wg, wu, wo, fg,
                 final_norm=(l == depth - 1)).reshape(b, s, d)
    return x
```
